```python
import math
import jax, jax.numpy as jnp
from jax import lax
import numpy as np

D_MODEL = 2048
BATCH = 4
SEQ = 4096
DEPTH = 1
DEC_BATCH = 32
DEC_SEQ = 16
PAST_LEN = 4096

CHUNK = 64
SB_HEADS = 8
HEAD_DIM = 128
ATT_DIM = SB_HEADS * HEAD_DIM
CONV_DIM = 1024
CONV_WIDTH = 3
Q_BLOCK = 128
PEER_HEADS = 8
PEER_QDIM = 256
PEER_HALF = PEER_QDIM // 2
N_KEYS = 128
N_EXPERTS = N_KEYS * N_KEYS
PEER_TOPK = 16
PEER_BLOCK = 256
N_MOD = 6
IN_COLS = 3 * ATT_DIM + 3 * CONV_DIM + 2 * D_MODEL
RMS_EPS = 1e-6

kernel_name = "stickbreak_shortconv_peer_stream_step"


def _rmsnorm(x, g):
    xf = x.astype(jnp.float32)
    r = lax.rsqrt(jnp.mean(xf * xf, axis=-1, keepdims=True) + RMS_EPS)
    return (xf * r).astype(x.dtype) * g


def _stick_breaking(q, k, v, q_pos, k_pos):
    z = jnp.einsum('bqhd,bkhd->bhqk', q, k).astype(jnp.float32) * (HEAD_DIM ** -0.5)
    mask = k_pos[None, :] < q_pos[:, None]
    log_keep = jnp.where(mask, jax.nn.log_sigmoid(-z), 0.0)
    later = lax.cumsum(log_keep, axis=3, reverse=True) - log_keep
    a = jnp.where(mask, jnp.exp(jax.nn.log_sigmoid(z) + later), 0.0)
    return jnp.einsum('bhqk,bkhd->bqhd', a.astype(v.dtype), v)


def _token_mixer(xn, w_in, conv_w, w_attn_proj, w_conv_out, w_o, past_k, past_v, conv_state):
    b, t, _ = xn.shape
    proj = xn @ w_in
    offs = [ATT_DIM, 2 * ATT_DIM, 3 * ATT_DIM,
            3 * ATT_DIM + CONV_DIM, 3 * ATT_DIM + 2 * CONV_DIM, 3 * ATT_DIM + 3 * CONV_DIM,
            3 * ATT_DIM + 3 * CONV_DIM + D_MODEL]
    q, k, v, hc, gb, gc, pre_ga, pre_gc = jnp.split(proj, offs, axis=-1)
    q = q.reshape(b, t, SB_HEADS, HEAD_DIM)
    k = k.reshape(b, t, SB_HEADS, HEAD_DIM)
    v = v.reshape(b, t, SB_HEADS, HEAD_DIM)
    if past_k is None:
        nb = t // Q_BLOCK
        k_pos = jnp.arange(t, dtype=jnp.int32)
        qb = q.reshape(b, nb, Q_BLOCK, SB_HEADS, HEAD_DIM).transpose(1, 0, 2, 3, 4)
        starts = jnp.arange(nb, dtype=jnp.int32) * Q_BLOCK
        o = lax.map(lambda a: _stick_breaking(a[0], k, v, a[1] + jnp.arange(Q_BLOCK, dtype=jnp.int32), k_pos),
                    (qb, starts))
        o = o.transpose(1, 0, 2, 3, 4).reshape(b, t, ATT_DIM)
        u_prev = jnp.zeros((b, CONV_WIDTH - 1, CONV_DIM), xn.dtype)
    else:
        p = past_k.shape[1]
        k_all = jnp.concatenate([past_k.astype(k.dtype), k], axis=1)
        v_all = jnp.concatenate([past_v.astype(v.dtype), v], axis=1)
        q_pos = p + jnp.arange(t, dtype=jnp.int32)
        k_pos = jnp.arange(p + t, dtype=jnp.int32)
        o = _stick_breaking(q, k_all, v_all, q_pos, k_pos).reshape(b, t, ATT_DIM)
        u_prev = conv_state.astype(xn.dtype)
    u = gc * hc
    upad = jnp.concatenate([u_prev, u], axis=1)
    conv = conv_w[0] * upad[:, 0:t]
    for i in range(1, CONV_WIDTH):
        conv = conv + conv_w[i] * upad[:, i:i + t]
    y_conv = (gb * conv) @ w_conv_out
    y_att = o @ w_attn_proj
    merged = jax.nn.sigmoid(pre_ga) * y_att + jax.nn.sigmoid(pre_gc) * y_conv
    return merged @ w_o, k, v, upad[:, -(CONV_WIDTH - 1):]


def _peer(h, w_query, sub_keys, expert_u, expert_v):
    b, t, d = h.shape
    n = b * t
    x2 = h.reshape(n, d)
    q = (x2 @ w_query).reshape(n, PEER_HEADS, 2, PEER_HALF)
    s = jnp.einsum('nhpd,hpkd->nhpk', q, sub_keys).astype(jnp.float32)
    s1, i1 = lax.top_k(s[:, :, 0], PEER_TOPK)
    s2, i2 = lax.top_k(s[:, :, 1], PEER_TOPK)
    cand = (s1[..., :, None] + s2[..., None, :]).reshape(n, PEER_HEADS, PEER_TOPK * PEER_TOPK)
    cidx = (i1[..., :, None] * N_KEYS + i2[..., None, :]).reshape(n, PEER_HEADS, PEER_TOPK * PEER_TOPK)
    top, pos = lax.top_k(cand, PEER_TOPK)
    eidx = jnp.take_along_axis(cidx, pos, axis=-1).reshape(n, PEER_HEADS * PEER_TOPK)
    gate = jax.nn.softmax(top, axis=-1).reshape(n, PEER_HEADS * PEER_TOPK).astype(h.dtype)
    nb = -(-n // PEER_BLOCK)
    pad = nb * PEER_BLOCK - n
    xb = jnp.pad(x2, ((0, pad), (0, 0))).reshape(nb, PEER_BLOCK, d)
    eb = jnp.pad(eidx, ((0, pad), (0, 0))).reshape(nb, PEER_BLOCK, PEER_HEADS * PEER_TOPK)
    gbk = jnp.pad(gate, ((0, pad), (0, 0))).reshape(nb, PEER_BLOCK, PEER_HEADS * PEER_TOPK)

    def body(a):
        xx, ee, gg = a
        pre = jnp.einsum('nd,ned->ne', xx, expert_u[ee])
        w = gg * jax.nn.gelu(pre, approximate=False)
        return jnp.einsum('ne,ned->nd', w, expert_v[ee])

    out = lax.map(body, (xb, eb, gbk)).reshape(nb * PEER_BLOCK, d)[:n]
    return out.reshape(b, t, d)


def _layer(x, c, past_k, past_v, conv_state, norm1_g, norm2_g, w_ada, b_ada, w_in, conv_w,
           w_attn_proj, w_conv_out, w_o, w_query, sub_keys, expert_u, expert_v):
    mod = jax.nn.silu(c) @ w_ada + b_ada
    sh1, sc1, g1, sh2, sc2, g2 = jnp.split(mod[:, None, :], N_MOD, axis=-1)
    h = _rmsnorm(x, norm1_g) * (1 + sc1) + sh1
    mix, k, v, st = _token_mixer(h, w_in, conv_w, w_attn_proj, w_conv_out, w_o, past_k, past_v, conv_state)
    x = x + g1 * mix
    h = _rmsnorm(x, norm2_g) * (1 + sc2) + sh2
    x = x + g2 * _peer(h, w_query, sub_keys, expert_u, expert_v)
    return x, k, v, st


def setup_inputs(seed: int = 0) -> dict:
    key = jax.random.key(seed)
    ks = jax.random.split(key, 24)
    f32 = jnp.float32
    nrm = lambda k, shape, s: jax.random.normal(k, shape, f32) * s
    return {
        "x_prompt": nrm(ks[0], (BATCH, SEQ, D_MODEL), 1.0),
        "x_sample": nrm(ks[1], (DEC_BATCH, DEC_SEQ, D_MODEL), 1.0),
        "c_prompt": nrm(ks[2], (BATCH, D_MODEL), 1.0),
        "c_sample": nrm(ks[3], (DEC_BATCH, D_MODEL), 1.0),
        "cache_k": nrm(ks[4], (DEPTH, DEC_BATCH, PAST_LEN, SB_HEADS, HEAD_DIM), 1.0),
        "cache_v": nrm(ks[5], (DEPTH, DEC_BATCH, PAST_LEN, SB_HEADS, HEAD_DIM), 1.0),
        "state_conv": nrm(ks[6], (DEPTH, DEC_BATCH, CONV_WIDTH - 1, CONV_DIM), 1.0),
        "norm1_g": 1.0 + nrm(ks[7], (DEPTH, D_MODEL), 0.02),
        "norm2_g": 1.0 + nrm(ks[8], (DEPTH, D_MODEL), 0.02),
        "w_ada": nrm(ks[9], (DEPTH, D_MODEL, N_MOD * D_MODEL), 0.5 * D_MODEL ** -0.5),
        "b_ada": nrm(ks[10], (DEPTH, N_MOD * D_MODEL), 0.01),
        "w_in": nrm(ks[11], (DEPTH, D_MODEL, IN_COLS), D_MODEL ** -0.5),
        "conv_w": nrm(ks[12], (DEPTH, CONV_WIDTH, CONV_DIM), CONV_WIDTH ** -0.5),
        "w_attn_proj": nrm(ks[13], (DEPTH, ATT_DIM, D_MODEL), ATT_DIM ** -0.5),
        "w_conv_out": nrm(ks[14], (DEPTH, CONV_DIM, D_MODEL), CONV_DIM ** -0.5),
        "w_o": nrm(ks[15], (DEPTH, D_MODEL, D_MODEL), D_MODEL ** -0.5),
        "w_query": nrm(ks[16], (DEPTH, D_MODEL, PEER_HEADS * PEER_QDIM), D_MODEL ** -0.5),
        "sub_keys": nrm(ks[17], (DEPTH, PEER_HEADS, 2, N_KEYS, PEER_HALF), PEER_HALF ** -0.5),
        "expert_u": nrm(ks[18], (DEPTH, N_EXPERTS, D_MODEL), D_MODEL ** -0.5),
        "expert_v": nrm(ks[19], (DEPTH, N_EXPERTS, D_MODEL), PEER_HEADS ** -0.5),
        "final_norm_g": 1.0 + nrm(ks[20], (D_MODEL,), 0.02),
    }


def reference(x_prompt, x_sample, c_prompt, c_sample, cache_k, cache_v, state_conv, norm1_g, norm2_g,
              w_ada, b_ada, w_in, conv_w, w_attn_proj, w_conv_out, w_o, w_query, sub_keys,
              expert_u, expert_v, final_norm_g):
    xp, xs = x_prompt, x_sample
    kp_l, vp_l, sp_l, ks_l, vs_l, ss_l = [], [], [], [], [], []
    for l in range(DEPTH):
        lw = (norm1_g[l], norm2_g[l], w_ada[l], b_ada[l], w_in[l], conv_w[l], w_attn_proj[l],
              w_conv_out[l], w_o[l], w_query[l], sub_keys[l], expert_u[l], expert_v[l])
        xp, kp, vp, sp = _layer(xp, c_prompt, None, None, None, *lw)
        xs, kk, vv, ss = _layer(xs, c_sample, cache_k[l], cache_v[l], state_conv[l], *lw)
        kp_l.append(kp); vp_l.append(vp); sp_l.append(sp)
        ks_l.append(kk); vs_l.append(vv); ss_l.append(ss)
    y_prompt = _rmsnorm(xp, final_norm_g)
    y_sample = _rmsnorm(xs, final_norm_g)
    return (y_prompt, y_sample, jnp.stack(kp_l), jnp.stack(vp_l), jnp.stack(sp_l),
            jnp.stack(ks_l), jnp.stack(vs_l), jnp.stack(ss_l))
```

```python
import jax
import jax.numpy as jnp
from jax import lax
from jax.experimental import pallas as pl
from jax.experimental.pallas import tpu as pltpu

F32 = jnp.float32
BF16 = jnp.bfloat16

RMS_EPS = 1e-6
HEAD_DIM = 128
ATT_BLOCK = 128
N_MOD = 6
PEER_TOPK = 16
N_KEYS = 128
GATHER_SLOTS = 4
VMEM_LIMIT = 56 * 1024 * 1024


def _cparams(*sem):
    return pltpu.CompilerParams(dimension_semantics=sem, vmem_limit_bytes=VMEM_LIMIT)


def _adaln_kernel(c_ref, w_ref, b_ref, o_ref):
    c = c_ref[...]
    a = (c * jax.nn.sigmoid(c)).astype(BF16)
    o_ref[...] = jnp.dot(a, w_ref[...].astype(BF16), preferred_element_type=F32) + b_ref[...]


def _adaln(c, w_ada, b_ada, tn=1024):
    nb, d = c.shape
    ncol = w_ada.shape[1]
    return pl.pallas_call(
        _adaln_kernel,
        grid=(ncol // tn,),
        in_specs=[pl.BlockSpec((nb, d), lambda j: (0, 0)),
                  pl.BlockSpec((d, tn), lambda j: (0, j)),
                  pl.BlockSpec((1, tn), lambda j: (0, j))],
        out_specs=pl.BlockSpec((nb, tn), lambda j: (0, j)),
        out_shape=jax.ShapeDtypeStruct((nb, ncol), F32),
        compiler_params=_cparams("parallel"),
        name="adaln",
    )(c, w_ada, b_ada.reshape(1, ncol))


def _modulated_norm(x, g, sc, sh):
    r = lax.rsqrt(jnp.mean(x * x, axis=-1, keepdims=True) + RMS_EPS)
    return (x * r) * g * (1.0 + sc) + sh


def _inproj_kernel(x_ref, sh_ref, sc_ref, g_ref, w_ref, o_ref, k_ref, v_ref, h_ref):
    tm = x_ref.shape[0]
    heads = o_ref.shape[1] // HEAD_DIM
    j = pl.program_id(2)

    @pl.when(j == 0)
    def _():
        h_ref[...] = _modulated_norm(x_ref[...], g_ref[...], sc_ref[...], sh_ref[...]).astype(BF16)

    res = jnp.dot(h_ref[...], w_ref[...], preferred_element_type=F32)
    o_ref[...] = res

    def to_cache_layout(dst_ref):
        for h in range(heads):
            dst_ref[pl.ds(h, tm, stride=heads), :] = res[:, h * HEAD_DIM:(h + 1) * HEAD_DIM]

    @pl.when(j == 1)
    def _():
        to_cache_layout(k_ref)

    @pl.when(j == 2)
    def _():
        to_cache_layout(v_ref)


def _mod_spec(arr, tm, ngrid):
    d = arr.shape[-1]
    rows = 1 if arr.shape[1] == 1 else tm
    if ngrid == 3:
        idx = (lambda b, t, j: (b, 0, 0)) if rows == 1 else (lambda b, t, j: (b, t, 0))
    else:
        idx = (lambda b, t: (b, 0, 0)) if rows == 1 else (lambda b, t: (b, t, 0))
    return pl.BlockSpec((None, rows, d), idx)


def _inproj(x, sh, sc, g, w_bf16, heads, tm):
    b, t, d = x.shape
    ncol = w_bf16.shape[1]
    tn = heads * HEAD_DIM
    kv_spec = pl.BlockSpec((None, tm * heads, HEAD_DIM), lambda b, t, j: (b, t, 0))
    kv_shape = jax.ShapeDtypeStruct((b, t * heads, HEAD_DIM), F32)
    return pl.pallas_call(
        _inproj_kernel,
        grid=(b, t // tm, ncol // tn),
        in_specs=[pl.BlockSpec((None, tm, d), lambda b, t, j: (b, t, 0)),
                  _mod_spec(sh, tm, 3), _mod_spec(sc, tm, 3),
                  pl.BlockSpec((1, d), lambda b, t, j: (0, 0)),
                  pl.BlockSpec((d, tn), lambda b, t, j: (0, j))],
        out_specs=[pl.BlockSpec((None, tm, tn), lambda b, t, j: (b, t, j)), kv_spec, kv_spec],
        out_shape=[jax.ShapeDtypeStruct((b, t, ncol), F32), kv_shape, kv_shape],
        scratch_shapes=[pltpu.VMEM((tm, d), BF16)],
        compiler_params=_cparams("parallel", "parallel", "arbitrary"),
        name="inproj",
    )(x, sh, sc, g.reshape(1, d), w_bf16)


def _cumsum_matrix():
    row = lax.broadcasted_iota(jnp.int32, (ATT_BLOCK, 2 * ATT_BLOCK), 0)
    col = lax.broadcasted_iota(jnp.int32, (ATT_BLOCK, 2 * ATT_BLOCK), 1)
    return jnp.where((row > col) | (col >= ATT_BLOCK), 1.0, 0.0).astype(BF16)


def _qk(q_bf16, k):
    return lax.dot_general(q_bf16, k.astype(BF16), (((1,), (1,)), ((), ())), preferred_element_type=F32)


def _sb_weights(z, tri, carry, mask):
    z = z * (HEAD_DIM ** -0.5)
    sp = jnp.maximum(z, 0.0) + jnp.log1p(jnp.exp(-jnp.abs(z)))
    log_keep = -sp
    if mask is not None:
        log_keep = jnp.where(mask, log_keep, 0.0)
    hi = log_keep.astype(BF16)
    lo = (log_keep - hi.astype(F32)).astype(BF16)
    cum = jnp.dot(hi, tri, preferred_element_type=F32) + jnp.dot(lo, tri, preferred_element_type=F32)
    a = jnp.exp((z - sp) + (cum[:, :ATT_BLOCK] + carry))
    if mask is not None:
        a = jnp.where(mask, a, 0.0)
    return a.astype(BF16), carry + cum[:, ATT_BLOCK:]


def _attn_prompt_kernel(q_ref, k_ref, v_ref, o_ref):
    nblk = q_ref.shape[0] // ATT_BLOCK
    tri = _cumsum_matrix()
    row = lax.broadcasted_iota(jnp.int32, (ATT_BLOCK, ATT_BLOCK), 0)
    col = lax.broadcasted_iota(jnp.int32, (ATT_BLOCK, ATT_BLOCK), 1)
    mask = col < row

    def block(q, rows, carry, acc, mask):
        a, carry = _sb_weights(_qk(q, k_ref[rows, :]), tri, carry, mask)
        return carry, acc + jnp.dot(a, v_ref[rows, :].astype(BF16), preferred_element_type=F32)

    def qblock(i, _):
        rows = pl.ds(pl.multiple_of(i * ATT_BLOCK, ATT_BLOCK), ATT_BLOCK)
        q = q_ref[rows, :].astype(BF16)
        zero = jnp.zeros((ATT_BLOCK, HEAD_DIM), F32)
        carry, acc = block(q, rows, zero, zero, mask)

        def kblock(jj, st):
            krows = pl.ds(pl.multiple_of((i - jj) * ATT_BLOCK, ATT_BLOCK), ATT_BLOCK)
            return block(q, krows, st[0], st[1], None)

        carry, acc = lax.fori_loop(1, i + 1, kblock, (carry, acc))
        o_ref[rows, :] = acc
        return 0

    lax.fori_loop(0, nblk, qblock, 0)


def _attn_prompt(proj, heads):
    b, t, _ = proj.shape
    att = heads * HEAD_DIM
    spec = lambda off: pl.BlockSpec((None, t, HEAD_DIM), lambda b, h, off=off: (b, 0, off + h))
    return pl.pallas_call(
        _attn_prompt_kernel,
        grid=(b, heads),
        in_specs=[spec(0), spec(heads), spec(2 * heads)],
        out_specs=pl.BlockSpec((None, t, HEAD_DIM), lambda b, h: (b, 0, h)),
        out_shape=jax.ShapeDtypeStruct((b, t, att), F32),
        compiler_params=_cparams("parallel", "parallel"),
        name="attn_prompt",
    )(proj, proj, proj)


def _attn_sample_kernel(q_ref, kn_ref, vn_ref, kc_ref, vc_ref, o_ref, carry_ref, acc_ref):
    tq = q_ref.shape[0]
    heads = q_ref.shape[1] // HEAD_DIM
    nblk = kc_ref.shape[0] // (heads * ATT_BLOCK)
    tri = _cumsum_matrix()
    qs = [q_ref[:, h * HEAD_DIM:(h + 1) * HEAD_DIM].astype(BF16) for h in range(heads)]

    def block(k_of, v_of, carry, acc, mask):
        z = jnp.concatenate([_qk(qs[h], k_of(h)) for h in range(heads)], axis=0)
        a, carry = _sb_weights(z, tri, carry, mask)
        pv = [jnp.dot(a[h * tq:(h + 1) * tq], v_of(h).astype(BF16), preferred_element_type=F32)
              for h in range(heads)]
        return carry, acc + jnp.concatenate(pv, axis=0)

    @pl.when(pl.program_id(1) == 0)
    def _():
        row = lax.broadcasted_iota(jnp.int32, (heads * tq, ATT_BLOCK), 0) % tq
        col = lax.broadcasted_iota(jnp.int32, (heads * tq, ATT_BLOCK), 1)
        zero = jnp.zeros((heads * tq, HEAD_DIM), F32)
        lanes = lambda h: slice(h * HEAD_DIM, (h + 1) * HEAD_DIM)
        carry, acc = block(lambda h: kn_ref[:, lanes(h)], lambda h: vn_ref[:, lanes(h)], zero, zero, col < row)
        carry_ref[...] = carry
        acc_ref[...] = acc

    def kblock(jj, st):
        start = (nblk - 1 - jj) * (ATT_BLOCK * heads)
        rows = lambda h: pl.ds(start + h, ATT_BLOCK, stride=heads)
        return block(lambda h: kc_ref[rows(h), :], lambda h: vc_ref[rows(h), :], st[0], st[1], None)

    carry, acc = lax.fori_loop(0, nblk, kblock, (carry_ref[...], acc_ref[...]))
    carry_ref[...] = carry
    acc_ref[...] = acc

    @pl.when(pl.program_id(1) == pl.num_programs(1) - 1)
    def _():
        for h in range(heads):
            o_ref[:, h * HEAD_DIM:(h + 1) * HEAD_DIM] = acc[h * tq:(h + 1) * tq]


def _attn_sample(proj, k_new_pad, v_new_pad, cache_k, cache_v, heads, chunk=1024):
    b, tq, _ = proj.shape
    att = heads * HEAD_DIM
    chunk = min(chunk, cache_k.shape[1] // heads)
    nchunk = cache_k.shape[1] // (chunk * heads)
    new_spec = pl.BlockSpec((None, ATT_BLOCK, att), lambda b, c: (b, 0, 0))
    cache_spec = pl.BlockSpec((None, chunk * heads, HEAD_DIM), lambda b, c: (b, nchunk - 1 - c, 0))
    return pl.pallas_call(
        _attn_sample_kernel,
        grid=(b, nchunk),
        in_specs=[pl.BlockSpec((None, tq, att), lambda b, c: (b, 0, 0)),
                  new_spec, new_spec, cache_spec, cache_spec],
        out_specs=pl.BlockSpec((None, tq, att), lambda b, c: (b, 0, 0)),
        out_shape=jax.ShapeDtypeStruct((b, tq, att), F32),
        scratch_shapes=[pltpu.VMEM((heads * tq, HEAD_DIM), F32), pltpu.VMEM((heads * tq, HEAD_DIM), F32)],
        compiler_params=_cparams("parallel", "arbitrary"),
        name="attn_sample",
    )(proj, k_new_pad, v_new_pad, cache_k, cache_v)


def _mixer_kernel(x_ref, o_ref, hc_ref, gb_ref, gc_ref, pga_ref, pgc_ref, cw_ref, st_ref, g1_ref,
                  wc_ref, wa_ref, wo_ref, x1_ref, nst_ref, uprev_ref):
    tm = x_ref.shape[0]

    @pl.when(pl.program_id(1) == 0)
    def _():
        uprev_ref[...] = st_ref[...]

    u = gc_ref[...] * hc_ref[...]
    prev = uprev_ref[...]
    row = lax.broadcasted_iota(jnp.int32, u.shape, 0)
    u1 = jnp.where(row == 0, prev[1:2], pltpu.roll(u, 1, 0))
    u2 = jnp.where(row == 0, prev[0:1], jnp.where(row == 1, prev[1:2], pltpu.roll(u, 2, 0)))
    cw = cw_ref[...]
    conv = cw[0:1] * u2 + cw[1:2] * u1 + cw[2:3] * u
    tail = u[tm - 2:tm, :]
    uprev_ref[...] = tail
    nst_ref[...] = tail

    y_conv = jnp.dot((gb_ref[...] * conv).astype(BF16), wc_ref[...], preferred_element_type=F32)
    y_att = jnp.dot(o_ref[...].astype(BF16), wa_ref[...], preferred_element_type=F32)
    merged = jax.nn.sigmoid(pga_ref[...]) * y_att + jax.nn.sigmoid(pgc_ref[...]) * y_conv
    mix = jnp.dot(merged.astype(BF16), wo_ref[...], preferred_element_type=F32)
    x1_ref[...] = x_ref[...] + g1_ref[...] * mix


def _mixer(x, o, proj, conv_w, state, g1, wc, wa, wo, tm):
    b, t, d = x.shape
    att = o.shape[-1]
    cdim = conv_w.shape[-1]
    assert att == cdim and d == 2 * cdim
    col = lambda width, k: pl.BlockSpec((None, tm, width), lambda b, t, k=k: (b, t, k))
    const = lambda shape: pl.BlockSpec(shape, lambda b, t: (0, 0), pipeline_mode=pl.Buffered(1))
    return pl.pallas_call(
        _mixer_kernel,
        grid=(b, t // tm),
        in_specs=[col(d, 0), col(att, 0),
                  col(cdim, 3), col(cdim, 4), col(cdim, 5),
                  col(d, 3), col(d, 4),
                  pl.BlockSpec((3, cdim), lambda b, t: (0, 0)),
                  pl.BlockSpec((None, 2, cdim), lambda b, t: (b, 0, 0)),
                  _mod_spec(g1, tm, 2),
                  const((cdim, d)), const((att, d)), const((d, d))],
        out_specs=[pl.BlockSpec((None, tm, d), lambda b, t: (b, t, 0)),
                   pl.BlockSpec((None, 2, cdim), lambda b, t: (b, 0, 0))],
        out_shape=[jax.ShapeDtypeStruct((b, t, d), F32), jax.ShapeDtypeStruct((b, 2, cdim), F32)],
        scratch_shapes=[pltpu.VMEM((2, cdim), F32)],
        compiler_params=_cparams("parallel", "arbitrary"),
        name="mixer",
    )(x, o, proj, proj, proj, proj, proj, conv_w, state, g1, wc, wa, wo)


def _topk_rows_step(s, index):
    m = jnp.max(s, axis=0, keepdims=True)
    idx = jnp.min(jnp.where(s == m, index, s.shape[0]), axis=0, keepdims=True)
    return m, idx, jnp.where(index == idx, -jnp.inf, s)


def _router_kernel(x_ref, sh_ref, sc_ref, g_ref, wq_ref, sk_ref, h_ref, eidx_ref, gate_ref,
                   s_ref, v_ref, i_ref, c_ref, ci_ref, tv_ref):
    tm = x_ref.shape[0]
    nhalf = sk_ref.shape[0]
    heads = nhalf // 2
    k = PEER_TOPK
    h2 = _modulated_norm(x_ref[...], g_ref[...], sc_ref[...], sh_ref[...])
    h_ref[...] = h2
    qp = jnp.dot(h2.astype(BF16), wq_ref[...], preferred_element_type=F32)

    for hp in range(nhalf):
        s_ref[hp] = lax.dot_general(sk_ref[hp].astype(BF16), qp[:, hp * N_KEYS:(hp + 1) * N_KEYS].astype(BF16),
                                    (((1,), (1,)), ((), ())), preferred_element_type=F32)

    key = lax.broadcasted_iota(jnp.int32, (N_KEYS, tm), 0)

    def half_step(it, _):
        for hp in range(nhalf):
            m, idx, s = _topk_rows_step(s_ref[hp], key)
            v_ref[hp, pl.ds(it, 1), :] = m
            i_ref[hp, pl.ds(it, 1), :] = idx
            s_ref[hp] = s
        return 0

    lax.fori_loop(0, k, half_step, 0)

    for h in range(heads):
        v1, v2 = v_ref[2 * h], v_ref[2 * h + 1]
        i1, i2 = i_ref[2 * h], i_ref[2 * h + 1]
        for a in range(k):
            c_ref[h, a * k:(a + 1) * k, :] = v1[a:a + 1] + v2
            ci_ref[h, a * k:(a + 1) * k, :] = i1[a:a + 1] * N_KEYS + i2

    cand = lax.broadcasted_iota(jnp.int32, (k * k, tm), 0)

    def cand_step(it, _):
        for h in range(heads):
            m, pos, c = _topk_rows_step(c_ref[h], cand)
            tv_ref[h, pl.ds(it, 1), :] = m
            eidx_ref[pl.ds(h * k + it, 1), :] = jnp.max(jnp.where(cand == pos, ci_ref[h], -1), axis=0, keepdims=True)
            c_ref[h] = c
        return 0

    lax.fori_loop(0, k, cand_step, 0)

    for h in range(heads):
        tv = tv_ref[h]
        ex = jnp.exp(tv - tv[0:1])
        gate_ref[h * k:(h + 1) * k, :] = ex / jnp.sum(ex, axis=0, keepdims=True)


def _router(x1, sh, sc, g, wq_bf16, sub_keys, tm):
    b, t, d = x1.shape
    nq = wq_bf16.shape[1]
    sk = sub_keys.reshape(-1, N_KEYS, sub_keys.shape[-1])
    nhalf = sk.shape[0]
    nsel = (nhalf // 2) * PEER_TOPK
    k = PEER_TOPK
    sel = pl.BlockSpec((None, nsel, tm), lambda b, t: (b, 0, t))
    return pl.pallas_call(
        _router_kernel,
        grid=(b, t // tm),
        in_specs=[pl.BlockSpec((None, tm, d), lambda b, t: (b, t, 0)),
                  _mod_spec(sh, tm, 2), _mod_spec(sc, tm, 2),
                  pl.BlockSpec((1, d), lambda b, t: (0, 0)),
                  pl.BlockSpec((d, nq), lambda b, t: (0, 0)),
                  pl.BlockSpec(sk.shape, lambda b, t: (0, 0, 0))],
        out_specs=[pl.BlockSpec((None, tm, d), lambda b, t: (b, t, 0)), sel, sel],
        out_shape=[jax.ShapeDtypeStruct((b, t, d), F32),
                   jax.ShapeDtypeStruct((b, nsel, t), jnp.int32),
                   jax.ShapeDtypeStruct((b, nsel, t), F32)],
        scratch_shapes=[pltpu.VMEM((nhalf, N_KEYS, tm), F32),
                        pltpu.VMEM((nhalf, k, tm), F32), pltpu.VMEM((nhalf, k, tm), jnp.int32),
                        pltpu.VMEM((nhalf // 2, k * k, tm), F32), pltpu.VMEM((nhalf // 2, k * k, tm), jnp.int32),
                        pltpu.VMEM((nhalf // 2, k, tm), F32)],
        compiler_params=_cparams("parallel", "parallel"),
        name="router",
    )(x1, sh, sc, g.reshape(1, d), wq_bf16, sk)


def _experts_kernel(idx_ref, gate_ref, h_ref, x1_ref, g2_ref, fg_ref, u_hbm, v_hbm, y_ref,
                    ubuf, vbuf, peer_ref, sem):
    nsel, tb = gate_ref.shape
    d = h_ref.shape[-1]
    ngrp = nsel // 8

    def issue(t, slot):
        for r in range(nsel):
            e = idx_ref[r, t]
            pltpu.make_async_copy(u_hbm.at[pl.ds(e, 1)], ubuf.at[slot, pl.ds(r, 1)], sem.at[0, slot]).start()
            pltpu.make_async_copy(v_hbm.at[pl.ds(e, 1)], vbuf.at[slot, pl.ds(r, 1)], sem.at[1, slot]).start()

    def wait(slot):
        pltpu.make_async_copy(u_hbm.at[pl.ds(0, nsel)], ubuf.at[slot], sem.at[0, slot]).wait()
        pltpu.make_async_copy(v_hbm.at[pl.ds(0, nsel)], vbuf.at[slot], sem.at[1, slot]).wait()

    tok = lax.broadcasted_iota(jnp.int32, (nsel, tb), 1)

    def compute(t, slot):
        x = h_ref[pl.ds(t, 1), :]
        gcol = jnp.sum(jnp.where(tok == t, gate_ref[...], 0.0), axis=-1, keepdims=True)
        acc = jnp.zeros((8, d), F32)
        for g in range(ngrp):
            rows = pl.ds(g * 8, 8)
            pre = jnp.sum(ubuf[slot, rows, :] * x, axis=-1, keepdims=True)
            gelu = 0.5 * pre * (1.0 + lax.erf(pre * (2.0 ** -0.5)))
            acc = acc + (gcol[g * 8:(g + 1) * 8] * gelu) * vbuf[slot, rows, :]
        peer_ref[pl.ds(t, 1), :] = jnp.sum(acc, axis=0, keepdims=True)

    def prologue(s, _):
        issue(s, s)
        return 0

    lax.fori_loop(0, GATHER_SLOTS, prologue, 0)

    def step(t, _):
        slot = lax.rem(t, GATHER_SLOTS)
        wait(slot)
        compute(t, slot)

        @pl.when(t + GATHER_SLOTS < tb)
        def _():
            issue(t + GATHER_SLOTS, slot)
        return 0

    lax.fori_loop(0, tb, step, 0)

    x2 = x1_ref[...] + g2_ref[...] * peer_ref[...]
    r = lax.rsqrt(jnp.mean(x2 * x2, axis=-1, keepdims=True) + RMS_EPS)
    y_ref[...] = (x2 * r) * fg_ref[...]


def _experts(eidx_t, gate_t, h2, x1, g2, fg, expert_u, expert_v, tb):
    b, t, d = x1.shape
    nsel = gate_t.shape[1]
    assert tb >= GATHER_SLOTS and t % tb == 0
    row = lambda width: pl.BlockSpec((None, tb, width), lambda b, t: (b, t, 0))
    return pl.pallas_call(
        _experts_kernel,
        grid=(b, t // tb),
        in_specs=[pl.BlockSpec((None, nsel, tb), lambda b, t: (b, 0, t), memory_space=pltpu.SMEM),
                  pl.BlockSpec((None, nsel, tb), lambda b, t: (b, 0, t)),
                  row(d), row(d), _mod_spec(g2, tb, 2),
                  pl.BlockSpec((1, d), lambda b, t: (0, 0)),
                  pl.BlockSpec(memory_space=pl.ANY), pl.BlockSpec(memory_space=pl.ANY)],
        out_specs=row(d),
        out_shape=jax.ShapeDtypeStruct((b, t, d), F32),
        scratch_shapes=[pltpu.VMEM((GATHER_SLOTS, nsel, d), F32),
                        pltpu.VMEM((GATHER_SLOTS, nsel, d), F32),
                        pltpu.VMEM((tb, d), F32),
                        pltpu.SemaphoreType.DMA((2, GATHER_SLOTS))],
        compiler_params=_cparams("arbitrary", "arbitrary"),
        name="experts",
    )(eidx_t, gate_t, h2, x1, g2, fg.reshape(1, d), expert_u, expert_v)


def _peer_and_norm(x1, sh2, sc2, g2, norm2_g, wq, sub_keys, expert_u, expert_v, final_g, tm_router, tb_experts):
    h2, eidx_t, gate_t = _router(x1, sh2, sc2, norm2_g, wq, sub_keys, tm_router)
    return _experts(eidx_t, gate_t, h2, x1, g2, final_g, expert_u, expert_v, tb_experts)


def kernel(x_prompt, x_sample, c_prompt, c_sample, cache_k, cache_v, state_conv, norm1_g, norm2_g, w_ada, b_ada,
           w_in, conv_w, w_attn_proj, w_conv_out, w_o, w_query, sub_keys, expert_u, expert_v, final_norm_g):
    depth = w_in.shape[0]
    assert depth == 1
    l = 0
    bp, tp, d = x_prompt.shape
    bs, ts, _ = x_sample.shape
    ns = bs * ts
    heads = cache_k.shape[3]
    att = heads * HEAD_DIM
    cdim = conv_w.shape[-1]
    past = cache_k.shape[2]

    mod = _adaln(jnp.concatenate([c_prompt, c_sample], axis=0), w_ada[l], b_ada[l])
    parts = jnp.split(mod, N_MOD, axis=-1)
    sh1p, sc1p, g1p, sh2p, sc2p, g2p = (m[:bp, None, :] for m in parts)
    sh1s, sc1s, g1s, sh2s, sc2s, g2s = (jnp.repeat(m[bp:], ts, axis=0)[None] for m in parts)

    win = w_in[l].astype(BF16)
    wa, wc, wo, wq = (w[l].astype(BF16) for w in (w_attn_proj, w_conv_out, w_o, w_query))
    peer = (norm2_g[l], wq, sub_keys[l], expert_u[l], expert_v[l], final_norm_g)

    proj, kp, vp = _inproj(x_prompt, sh1p, sc1p, norm1_g[l], win, heads, tm=512)
    o = _attn_prompt(proj, heads)
    x1, sp = _mixer(x_prompt, o, proj, conv_w[l], jnp.zeros((bp, 2, cdim), F32), g1p, wc, wa, wo, tm=256)
    y_prompt = _peer_and_norm(x1, sh2p, sc2p, g2p, *peer, tm_router=256, tb_experts=128)

    proj, ks, vs = _inproj(x_sample.reshape(1, ns, d), sh1s, sc1s, norm1_g[l], win, heads, tm=ns)
    proj = proj.reshape(bs, ts, -1)
    pad = ((0, 0), (0, ATT_BLOCK - ts), (0, 0))
    k_new = jnp.pad(proj[:, :, att:2 * att], pad)
    v_new = jnp.pad(proj[:, :, 2 * att:3 * att], pad)
    o = _attn_sample(proj, k_new, v_new, cache_k[l].reshape(bs, past * heads, HEAD_DIM),
                     cache_v[l].reshape(bs, past * heads, HEAD_DIM), heads)
    x1, ss = _mixer(x_sample, o, proj, conv_w[l], state_conv[l], g1s.reshape(bs, ts, d), wc, wa, wo, tm=ts)
    y_sample = _peer_and_norm(x1.reshape(1, ns, d), sh2s, sc2s, g2s, *peer,
                              tm_router=min(256, ns), tb_experts=min(128, ns)).reshape(bs, ts, d)

    return (y_prompt, y_sample,
            kp.reshape(depth, bp, tp, heads, HEAD_DIM), vp.reshape(depth, bp, tp, heads, HEAD_DIM),
            sp.reshape(depth, bp, 2, cdim),
            ks.reshape(depth, bs, ts, heads, HEAD_DIM), vs.reshape(depth, bs, ts, heads, HEAD_DIM),
            ss.reshape(depth, bs, 2, cdim))
```
